```python
import functools
import jax, jax.numpy as jnp
from jax import lax
import numpy as np

D_MODEL = 2048
BATCH = 16
SEQ = 2048
DEPTH = 1
DEC_BATCH = 128
DEC_SEQ = 8
PAST_LEN = 16384
PAGE_SIZE = 128

MIX_W = D_MODEL
ATT_W = MIX_W // 2
SGU_W = MIX_W - ATT_W
HEAD_DIM = 64
N_HEADS = ATT_W // HEAD_DIM
N_KV_HEADS = 2
GROUP = N_HEADS // N_KV_HEADS
KV_W = N_KV_HEADS * HEAD_DIM
ROT_DIM = HEAD_DIM // 4
ROPE_THETA = 500000.0
WINDOW = 128
ATT_BLOCK = WINDOW
SGU_HEADS = 8
SGU_HD = SGU_W // SGU_HEADS
CHUNK = 128
IN_DIM = ATT_W + 2 * KV_W + 2 * SGU_W
N_KEYS = 128
N_EXPERTS = N_KEYS * N_KEYS
PEER_HEADS = 8
PEER_QDIM = 256
PEER_HALF = PEER_QDIM // 2
PEER_TOPK = 16
PEER_BLOCK = 128
PLE_DIM = 256
EPS = 1e-6

kernel_name = 'hymba_swa_sgu_peer_decoder_step'


def rmsnorm(x, g):
    xf = x.astype(jnp.float32)
    y = xf * lax.rsqrt(jnp.mean(xf * xf, axis=-1, keepdims=True) + EPS)
    return (y * g.astype(jnp.float32)).astype(x.dtype)


def rope(x, pos):
    half = ROT_DIM // 2
    inv = jnp.power(jnp.float32(ROPE_THETA), -jnp.arange(half, dtype=jnp.float32) * 2.0 / ROT_DIM)
    ang = pos[:, None] * inv[None, :]
    cos = jnp.cos(ang)[:, None, :]
    sin = jnp.sin(ang)[:, None, :]
    xr = x[..., :ROT_DIM].astype(jnp.float32)
    x1, x2 = xr[..., :half], xr[..., half:]
    rot = jnp.concatenate([x1 * cos - x2 * sin, x2 * cos + x1 * sin], axis=-1).astype(x.dtype)
    return jnp.concatenate([rot, x[..., ROT_DIM:]], axis=-1)


def sink_attention(q, k, v, mask, sinks):
    s = jnp.einsum('...qkgd,...skd->...kgqs', q, k).astype(jnp.float32) * (HEAD_DIM ** -0.5)
    s = jnp.where(mask, s, -jnp.inf)
    sink = jnp.broadcast_to(sinks.astype(jnp.float32).reshape(N_KV_HEADS, GROUP)[:, :, None, None],
                            s.shape[:-1] + (1,))
    pr = jax.nn.softmax(jnp.concatenate([s, sink], axis=-1), axis=-1)[..., :-1].astype(v.dtype)
    return jnp.einsum('...kgqs,...skd->...qkgd', pr, v)


def attend_prompt(q, k, v, sinks):
    B, S = q.shape[0], q.shape[1]
    nb = S // ATT_BLOCK
    qb = q.reshape(B, nb, ATT_BLOCK, N_KV_HEADS, GROUP, HEAD_DIM)

    def with_prev(t):
        tb = t.reshape(B, nb, ATT_BLOCK, N_KV_HEADS, HEAD_DIM)
        prev = jnp.pad(tb, ((0, 0), (1, 0), (0, 0), (0, 0), (0, 0)))[:, :-1]
        return jnp.concatenate([prev, tb], axis=2)

    qi = jnp.arange(ATT_BLOCK)[:, None]
    sj = jnp.arange(2 * ATT_BLOCK)[None, :]
    rel = qi + ATT_BLOCK - sj
    band = (rel >= 0) & (rel < WINDOW)
    not_first = jnp.arange(nb)[:, None, None] > 0
    mask = (band[None] & (not_first | (sj >= ATT_BLOCK)[None]))[:, None, None]
    o = sink_attention(qb, with_prev(k), with_prev(v), mask, sinks)
    return o.reshape(B, S, N_KV_HEADS, GROUP, HEAD_DIM), k[:, -WINDOW:], v[:, -WINDOW:]


def attend_sample(q, k, v, sinks, cache_k, cache_v):
    T = q.shape[1]
    C = cache_k.shape[1]
    kc = jnp.concatenate([cache_k, k], axis=1)
    vc = jnp.concatenate([cache_v, v], axis=1)
    i = jnp.arange(T)[:, None]
    j = jnp.arange(C + T)[None, :]
    rel = i + C - j
    mask = (rel >= 0) & (rel < WINDOW)
    o = sink_attention(q, kc, vc, mask, sinks)
    return o, kc[:, T:], vc[:, T:]


def sgu(u, v, w, b):
    B, L = u.shape[0], u.shape[1]
    c = min(L, CHUNK)
    wm = w[:, :c, :c] * jnp.tril(jnp.ones((c, c), dtype=w.dtype))
    vc = v.reshape(B, L // c, c, SGU_HEADS, SGU_HD)
    z = jnp.einsum('hts,bnshd->bnthd', wm, vc) + b[:, :c].T[:, :, None]
    return u * z.reshape(B, L, SGU_HEADS, SGU_HD)


def peer(x, wq, k1, k2, u_tab, v_tab):
    B, L, D = x.shape
    n = B * L
    pad = (-n) % PEER_BLOCK
    xt = jnp.pad(x.reshape(n, D), ((0, pad), (0, 0))).reshape(-1, PEER_BLOCK, D)

    def one_block(xb):
        q = jnp.einsum('td,de->te', xb, wq).reshape(PEER_BLOCK, PEER_HEADS, 2, PEER_HALF)
        s1 = jnp.einsum('thc,nc->thn', q[:, :, 0], k1).astype(jnp.float32)
        s2 = jnp.einsum('thc,nc->thn', q[:, :, 1], k2).astype(jnp.float32)
        v1, i1 = lax.top_k(s1, PEER_TOPK)
        v2, i2 = lax.top_k(s2, PEER_TOPK)
        cand = (v1[..., :, None] + v2[..., None, :]).reshape(PEER_BLOCK, PEER_HEADS, PEER_TOPK * PEER_TOPK)
        sc, ci = lax.top_k(cand, PEER_TOPK)
        e = (jnp.take_along_axis(i1, ci // PEER_TOPK, axis=-1) * N_KEYS
             + jnp.take_along_axis(i2, ci % PEER_TOPK, axis=-1))
        g = jax.nn.softmax(sc, axis=-1).astype(xb.dtype)
        ue = jnp.take(u_tab, e, axis=0)
        act = jax.nn.gelu(jnp.einsum('thkd,td->thk', ue, xb))
        ve = jnp.take(v_tab, e, axis=0)
        return jnp.einsum('thk,thkd->td', g * act, ve)

    out = lax.map(one_block, xt)
    return out.reshape(-1, D)[:n].reshape(B, L, D)


def trunk_layer(x, ple, pos, attend, lw):
    (norm_mix, w_in, q_norm, k_norm, sinks, sgu_norm, sgu_w, sgu_b, attn_out_norm, sgu_out_norm,
     w_out, norm_ffn, peer_wq, peer_k1, peer_k2, peer_u, peer_v, ple_w, ple_gate_norm, ple_gate_w) = lw
    B, L, _ = x.shape
    hn = rmsnorm(x, norm_mix)
    proj = jnp.einsum('bld,de->ble', hn, w_in)
    o1 = ATT_W
    o2 = o1 + KV_W
    o3 = o2 + KV_W
    o4 = o3 + SGU_W
    q = rope(rmsnorm(proj[..., :o1].reshape(B, L, N_HEADS, HEAD_DIM), q_norm), pos)
    k = rope(rmsnorm(proj[..., o1:o2].reshape(B, L, N_KV_HEADS, HEAD_DIM), k_norm), pos)
    v = proj[..., o2:o3].reshape(B, L, N_KV_HEADS, HEAD_DIM)
    su = jax.nn.gelu(proj[..., o3:o4]).reshape(B, L, SGU_HEADS, SGU_HD)
    sv = rmsnorm(jax.nn.gelu(proj[..., o4:]).reshape(B, L, SGU_HEADS, SGU_HD), sgu_norm)
    attn, k_state, v_state = attend(q.reshape(B, L, N_KV_HEADS, GROUP, HEAD_DIM), k, v, sinks)
    attn = attn.reshape(B, L, ATT_W)
    sg = sgu(su, sv, sgu_w, sgu_b).reshape(B, L, SGU_W)
    mixed = jnp.concatenate([rmsnorm(attn, attn_out_norm), rmsnorm(sg, sgu_out_norm)], axis=-1)
    h = x + jnp.einsum('ble,ed->bld', mixed, w_out)
    h = h + peer(rmsnorm(h, norm_ffn), peer_wq, peer_k1, peer_k2, peer_u, peer_v)
    gate = jax.nn.sigmoid(jnp.einsum('bld,de->ble', rmsnorm(h, ple_gate_norm), ple_gate_w))
    h = h + jnp.einsum('blp,pd->bld', ple, ple_w) * gate
    return h, k_state, v_state, sv


def setup_inputs(seed: int = 0) -> dict:
    key = jax.random.key(seed)
    ks = jax.random.split(key, 32)
    f32 = jnp.float32
    C = min(WINDOW, PAST_LEN)

    def nrm(k, shape, scale):
        return jax.random.normal(k, shape, f32) * scale

    def gain(k, n):
        return 1.0 + 0.1 * jax.random.normal(k, (DEPTH, n), f32)

    return {
        'x_prompt': nrm(ks[0], (BATCH, SEQ, D_MODEL), 1.0),
        'x_sample': nrm(ks[1], (DEC_BATCH, DEC_SEQ, D_MODEL), 1.0),
        'cache_k': nrm(ks[2], (DEPTH, DEC_BATCH, C, N_KV_HEADS, HEAD_DIM), 1.0),
        'cache_v': nrm(ks[3], (DEPTH, DEC_BATCH, C, N_KV_HEADS, HEAD_DIM), 1.0),
        'p_prompt': nrm(ks[4], (DEPTH, BATCH, SEQ, PLE_DIM), 1.0),
        'p_sample': nrm(ks[5], (DEPTH, DEC_BATCH, DEC_SEQ, PLE_DIM), 1.0),
        'norm_mix': gain(ks[6], D_MODEL),
        'w_in': nrm(ks[7], (DEPTH, D_MODEL, IN_DIM), D_MODEL ** -0.5),
        'q_norm': gain(ks[8], HEAD_DIM),
        'k_norm': gain(ks[9], HEAD_DIM),
        'sinks': nrm(ks[10], (DEPTH, N_HEADS), 1.0),
        'sgu_norm': gain(ks[11], SGU_HD),
        'sgu_w': nrm(ks[12], (DEPTH, SGU_HEADS, CHUNK, CHUNK), CHUNK ** -0.5),
        'sgu_b': 1.0 + 0.1 * jax.random.normal(ks[13], (DEPTH, SGU_HEADS, CHUNK), f32),
        'attn_out_norm': gain(ks[14], ATT_W),
        'sgu_out_norm': gain(ks[15], SGU_W),
        'w_out': nrm(ks[16], (DEPTH, MIX_W, D_MODEL), MIX_W ** -0.5),
        'norm_ffn': gain(ks[17], D_MODEL),
        'peer_wq': nrm(ks[18], (DEPTH, D_MODEL, PEER_HEADS * PEER_QDIM), D_MODEL ** -0.5),
        'peer_k1': nrm(ks[19], (DEPTH, N_KEYS, PEER_HALF), PEER_HALF ** -0.5),
        'peer_k2': nrm(ks[20], (DEPTH, N_KEYS, PEER_HALF), PEER_HALF ** -0.5),
        'peer_u': nrm(ks[21], (DEPTH, N_EXPERTS, D_MODEL), D_MODEL ** -0.5),
        'peer_v': nrm(ks[22], (DEPTH, N_EXPERTS, D_MODEL), PEER_HEADS ** -0.5),
        'ple_w': nrm(ks[23], (DEPTH, PLE_DIM, D_MODEL), PLE_DIM ** -0.5),
        'ple_gate_norm': gain(ks[24], D_MODEL),
        'ple_gate_w': nrm(ks[25], (DEPTH, D_MODEL, D_MODEL), D_MODEL ** -0.5),
    }


def reference(x_prompt, x_sample, cache_k, cache_v, p_prompt, p_sample, norm_mix, w_in, q_norm, k_norm,
              sinks, sgu_norm, sgu_w, sgu_b, attn_out_norm, sgu_out_norm, w_out, norm_ffn, peer_wq,
              peer_k1, peer_k2, peer_u, peer_v, ple_w, ple_gate_norm, ple_gate_w):
    pos_p = jnp.arange(x_prompt.shape[1], dtype=jnp.float32)
    pos_s = PAST_LEN + jnp.arange(x_sample.shape[1], dtype=jnp.float32)
    hp, hs = x_prompt, x_sample
    kp_l, vp_l, ks_l, vs_l, sv_l = [], [], [], [], []
    for i in range(DEPTH):
        lw = (norm_mix[i], w_in[i], q_norm[i], k_norm[i], sinks[i], sgu_norm[i], sgu_w[i], sgu_b[i],
              attn_out_norm[i], sgu_out_norm[i], w_out[i], norm_ffn[i], peer_wq[i], peer_k1[i],
              peer_k2[i], peer_u[i], peer_v[i], ple_w[i], ple_gate_norm[i], ple_gate_w[i])
        hp, kp, vp, _ = trunk_layer(hp, p_prompt[i], pos_p, attend_prompt, lw)
        attend_s = functools.partial(attend_sample, cache_k=cache_k[i], cache_v=cache_v[i])
        hs, kss, vss, svs = trunk_layer(hs, p_sample[i], pos_s, attend_s, lw)
        kp_l.append(kp)
        vp_l.append(vp)
        ks_l.append(kss)
        vs_l.append(vss)
        sv_l.append(svs)
    win_k_prompt = jnp.stack(kp_l)
    win_v_prompt = jnp.stack(vp_l)
    win_k_sample = jnp.stack(ks_l)
    win_v_sample = jnp.stack(vs_l)
    sgu_v_sample = jnp.stack(sv_l)
    return (hp, hs, win_k_prompt, win_v_prompt, win_k_sample, win_v_sample, sgu_v_sample)
```

```python
import functools

import jax
import jax.numpy as jnp
from jax import lax
from jax.experimental import pallas as pl
from jax.experimental.pallas import tpu as pltpu

F32 = jnp.float32
BF16 = jnp.bfloat16

EPS = 1e-6
PAST_LEN = 16384
ROPE_THETA = 500000.0
HEAD_DIM = 64
ROT_DIM = HEAD_DIM // 4
WINDOW = 128
CHUNK = 128
N_KEYS = 128
PEER_HEADS = 8
PEER_TOPK = 16
PEER_HALF = 128

LANES = 128
VMEM_LIMIT = 56 * 1024 * 1024

TM_IN = 256
ATT_BLOCK = 128
PEER_TB = 16
NEG_INF = float("-inf")


def _gelu(x):
    return 0.5 * x * (1.0 + jnp.tanh(0.7978845608028654 * (x + 0.044715 * (x * x * x))))


def _rms(x, gain):
    return x * lax.rsqrt(jnp.mean(x * x, axis=-1, keepdims=True) + EPS) * gain


def _dot(a, b):
    return jnp.dot(a, b, preferred_element_type=F32)


def _dot_nt(a, b):
    return lax.dot_general(a, b, (((1,), (1,)), ((), ())), preferred_element_type=F32)


def _dot_split(a, b01):
    hi = a.astype(BF16)
    lo = (a - hi.astype(F32)).astype(BF16)
    return _dot(hi, b01) + _dot(lo, b01)


def _const_spec(shape):
    nd = len(shape)
    return pl.BlockSpec(shape, lambda *_: (0,) * nd)


def _mix_in_kernel(x_ref, nm_ref, w_ref, gain_ref, ind_ref, indt_ref, c_ref, s1_ref, s2_ref, sn_ref,
                   q_ref, k_ref, v_ref, su_ref, sv_ref, *, att_w, kv_w, sgu_w):
    x = x_ref[...]
    hn = _rms(x, nm_ref[...])
    proj = _dot(hn.astype(BF16), w_ref[...])
    qk_w = att_w + kv_w
    o3 = qk_w + kv_w
    o4 = o3 + sgu_w
    qk = proj[:, :qk_w]
    ssq = _dot_split(qk * qk, ind_ref[...])
    inv = lax.rsqrt(ssq * (1.0 / HEAD_DIM) + EPS)
    y = qk * _dot_split(inv, indt_ref[...]) * gain_ref[...]
    c = c_ref[...]
    s1 = s1_ref[...]
    s2 = s2_ref[...]
    parts = []
    for j in range(qk_w // LANES):
        yj = y[:, j * LANES:(j + 1) * LANES]
        parts.append(yj * c + pltpu.roll(yj, LANES - ROT_DIM // 2, 1) * s1 + pltpu.roll(yj, ROT_DIM // 2, 1) * s2)
    for j in range(att_w // LANES):
        q_ref[:, j * LANES:(j + 1) * LANES] = parts[j].astype(q_ref.dtype)
    for j in range(kv_w // LANES):
        k_ref[:, j * LANES:(j + 1) * LANES] = parts[att_w // LANES + j]
    v_ref[...] = proj[:, qk_w:o3]
    su_ref[...] = _gelu(proj[:, o3:o4])
    sn = sn_ref[...]
    for h in range(sgu_w // LANES):
        blk = _gelu(proj[:, o4 + h * LANES:o4 + (h + 1) * LANES])
        sv_ref[:, h * LANES:(h + 1) * LANES] = _rms(blk, sn)


def _mix_in(x2, norm_mix, w_in_bf, gain_qk, ind, indt, rope_c, rope_s1, rope_s2, sgu_norm, *, tm, rope_blocks,
            att_w, kv_w, sgu_w):
    n, d = x2.shape
    in_dim = w_in_bf.shape[1]
    qk_w = att_w + kv_w
    grid = (n // tm,)
    row = lambda i: (i, 0)
    rope_map = (lambda i: (i % rope_blocks, 0)) if rope_blocks > 1 else (lambda i: (0, 0))
    kern = functools.partial(_mix_in_kernel, att_w=att_w, kv_w=kv_w, sgu_w=sgu_w)
    return pl.pallas_call(
        kern,
        grid=grid,
        in_specs=[
            pl.BlockSpec((tm, d), row),
            _const_spec((1, d)),
            _const_spec((d, in_dim)),
            _const_spec((1, qk_w)),
            _const_spec((qk_w, LANES)),
            _const_spec((LANES, qk_w)),
            pl.BlockSpec((tm, LANES), rope_map),
            pl.BlockSpec((tm, LANES), rope_map),
            pl.BlockSpec((tm, LANES), rope_map),
            _const_spec((1, LANES)),
        ],
        out_specs=[
            pl.BlockSpec((tm, att_w), row),
            pl.BlockSpec((tm, kv_w), row),
            pl.BlockSpec((tm, kv_w), row),
            pl.BlockSpec((tm, sgu_w), row),
            pl.BlockSpec((tm, sgu_w), row),
        ],
        out_shape=[
            jax.ShapeDtypeStruct((n, att_w), BF16),
            jax.ShapeDtypeStruct((n, kv_w), F32),
            jax.ShapeDtypeStruct((n, kv_w), F32),
            jax.ShapeDtypeStruct((n, sgu_w), F32),
            jax.ShapeDtypeStruct((n, sgu_w), F32),
        ],
        compiler_params=pltpu.CompilerParams(dimension_semantics=("arbitrary",), vmem_limit_bytes=VMEM_LIMIT),
        name="mix_in",
    )(x2, norm_mix, w_in_bf, gain_qk, ind, indt, rope_c, rope_s1, rope_s2, sgu_norm)


def _dup_heads(kv2):
    lane = lax.broadcasted_iota(jnp.int32, kv2.shape, 1)
    rolled = pltpu.roll(kv2, HEAD_DIM, 1)
    lo = lane < HEAD_DIM
    return jnp.where(lo, kv2, rolled), jnp.where(lo, rolled, kv2)


def _sink_attention(q_ref, kdup, vdup, valid, sinks_ref, n_heads, group):
    rows = valid.shape[0]
    lane = lax.broadcasted_iota(jnp.int32, (rows, LANES), 1)
    lo = lane < HEAD_DIM
    zero = jnp.zeros((rows, LANES), BF16)
    outs = []
    for pair in range(n_heads // 2):
        kv = (2 * pair) // group
        q_pair = q_ref[:, pair * LANES:(pair + 1) * LANES]
        halves = []
        for half in range(2):
            head = 2 * pair + half
            qm = jnp.where(lo if half == 0 else jnp.logical_not(lo), q_pair, zero)
            s = _dot_nt(qm, kdup[kv]) * (HEAD_DIM ** -0.5)
            s = jnp.where(valid, s, NEG_INF)
            sink = sinks_ref[head]
            m = jnp.maximum(jnp.max(s, axis=-1, keepdims=True), sink)
            e = jnp.exp(s - m)
            denom = jnp.sum(e, axis=-1, keepdims=True) + jnp.exp(sink - m)
            p = (e / denom).astype(BF16)
            halves.append(_dot(p, vdup[kv]))
        outs.append(jnp.where(lo, halves[0], halves[1]))
    return jnp.concatenate(outs, axis=1)


def _sgu(su_ref, sv_ref, wm_ref, bt_ref, n_heads):
    outs = []
    for h in range(n_heads):
        sl = slice(h * LANES, (h + 1) * LANES)
        z = _dot(wm_ref[h], sv_ref[:, sl].astype(BF16)) + bt_ref[:, h:h + 1]
        outs.append(su_ref[:, sl] * z)
    return jnp.concatenate(outs, axis=1)


def _mix_tail(attn, sg, x_ref, aon_ref, son_ref, wout_ref, nffn_ref, h_ref, hn_ref):
    mixed = jnp.concatenate([_rms(attn, aon_ref[...]), _rms(sg, son_ref[...])], axis=1).astype(BF16)
    h = x_ref[...] + _dot(mixed, wout_ref[...])
    h_ref[...] = h
    hn_ref[...] = _rms(h, nffn_ref[...]).astype(hn_ref.dtype)


def _mix_out_prompt_kernel(sinks_ref, q_ref, kc_ref, kp_ref, vc_ref, vp_ref, su_ref, sv_ref, x_ref, wm_ref, bt_ref,
                           aon_ref, son_ref, wout_ref, nffn_ref, h_ref, hn_ref, *, n_heads, group, sgu_heads):
    j = pl.program_id(1)
    k2 = jnp.concatenate([kp_ref[...], kc_ref[...]], axis=0).astype(BF16)
    v2 = jnp.concatenate([vp_ref[...], vc_ref[...]], axis=0).astype(BF16)
    kdup = _dup_heads(k2)
    vdup = _dup_heads(v2)
    row = lax.broadcasted_iota(jnp.int32, (ATT_BLOCK, 2 * ATT_BLOCK), 0)
    col = lax.broadcasted_iota(jnp.int32, (ATT_BLOCK, 2 * ATT_BLOCK), 1)
    in_prev = col < ATT_BLOCK
    valid = jnp.logical_or(jnp.logical_and(jnp.logical_and(in_prev, col > row), j > 0),
                           jnp.logical_and(jnp.logical_not(in_prev), col - ATT_BLOCK <= row))
    attn = _sink_attention(q_ref, kdup, vdup, valid, sinks_ref, n_heads, group)
    sg = _sgu(su_ref, sv_ref, wm_ref, bt_ref, sgu_heads)
    _mix_tail(attn, sg, x_ref, aon_ref, son_ref, wout_ref, nffn_ref, h_ref, hn_ref)


def _mix_out_prompt(sinks, q, k, v, su, sv, x2, wm_bf, bt, aon, son, w_out_bf, nffn, *, batch, seq, n_heads, group,
                    sgu_heads):
    n, d = x2.shape
    att_w = q.shape[1]
    kv_w = k.shape[1]
    sgu_w = su.shape[1]
    nb = seq // ATT_BLOCK
    cur = lambda b, j: (b * nb + j, 0)
    prev = lambda b, j: (b * nb + jnp.maximum(j - 1, 0), 0)
    kern = functools.partial(_mix_out_prompt_kernel, n_heads=n_heads, group=group, sgu_heads=sgu_heads)
    return pl.pallas_call(
        kern,
        grid=(batch, nb),
        in_specs=[
            pl.BlockSpec(memory_space=pltpu.SMEM),
            pl.BlockSpec((ATT_BLOCK, att_w), cur),
            pl.BlockSpec((ATT_BLOCK, kv_w), cur),
            pl.BlockSpec((ATT_BLOCK, kv_w), prev),
            pl.BlockSpec((ATT_BLOCK, kv_w), cur),
            pl.BlockSpec((ATT_BLOCK, kv_w), prev),
            pl.BlockSpec((ATT_BLOCK, sgu_w), cur),
            pl.BlockSpec((ATT_BLOCK, sgu_w), cur),
            pl.BlockSpec((ATT_BLOCK, d), cur),
            _const_spec(wm_bf.shape),
            _const_spec(bt.shape),
            _const_spec((1, att_w)),
            _const_spec((1, sgu_w)),
            _const_spec(w_out_bf.shape),
            _const_spec((1, d)),
        ],
        out_specs=[pl.BlockSpec((ATT_BLOCK, d), cur), pl.BlockSpec((ATT_BLOCK, d), cur)],
        out_shape=[jax.ShapeDtypeStruct((n, d), F32), jax.ShapeDtypeStruct((n, d), BF16)],
        compiler_params=pltpu.CompilerParams(dimension_semantics=("arbitrary", "arbitrary"),
                                             vmem_limit_bytes=VMEM_LIMIT),
        name="mix_out_prompt",
    )(sinks, q, k, k, v, v, su, sv, x2, wm_bf, bt, aon, son, w_out_bf, nffn)


def _mix_out_sample_kernel(sinks_ref, q_ref, kn_ref, vn_ref, ck_ref, cv_ref, su_ref, sv_ref, x_ref, wm_ref, bt_ref,
                           aon_ref, son_ref, wout_ref, nffn_ref, h_ref, hn_ref, wk_ref, wv_ref, *, n_heads, group,
                           sgu_heads, seqs, t_new, cache_len):
    rows = seqs * t_new
    n_cache = seqs * cache_len
    k_all = jnp.concatenate([ck_ref[...], kn_ref[...]], axis=0).astype(BF16)
    v_all = jnp.concatenate([cv_ref[...], vn_ref[...]], axis=0).astype(BF16)
    kdup = _dup_heads(k_all)
    vdup = _dup_heads(v_all)
    shape = (rows, n_cache + rows)
    row = lax.broadcasted_iota(jnp.int32, shape, 0)
    col = lax.broadcasted_iota(jnp.int32, shape, 1)
    r_seq = row // t_new
    r_pos = row % t_new
    in_cache = col < n_cache
    cn = col - n_cache
    c_seq = jnp.where(in_cache, col // cache_len, cn // t_new)
    ok = jnp.logical_or(jnp.logical_and(in_cache, (col % cache_len) > r_pos + (cache_len - WINDOW)),
                        jnp.logical_and(jnp.logical_not(in_cache), (cn % t_new) <= r_pos))
    valid = jnp.logical_and(c_seq == r_seq, ok)
    attn = _sink_attention(q_ref, kdup, vdup, valid, sinks_ref, n_heads, group)
    sg = _sgu(su_ref, sv_ref, wm_ref, bt_ref, sgu_heads)
    _mix_tail(attn, sg, x_ref, aon_ref, son_ref, wout_ref, nffn_ref, h_ref, hn_ref)
    keep = cache_len - t_new
    for s in range(seqs):
        base = s * cache_len
        wk_ref[base:base + keep, :] = ck_ref[base + t_new:base + cache_len, :]
        wk_ref[base + keep:base + cache_len, :] = kn_ref[s * t_new:(s + 1) * t_new, :]
        wv_ref[base:base + keep, :] = cv_ref[base + t_new:base + cache_len, :]
        wv_ref[base + keep:base + cache_len, :] = vn_ref[s * t_new:(s + 1) * t_new, :]


def _mix_out_sample(sinks, q, k, v, ck, cv, su, sv, x2, wbig_bf, bt, aon, son, w_out_bf, nffn, *, n_heads, group,
                    sgu_heads, seqs, t_new, cache_len):
    n, d = x2.shape
    att_w = q.shape[1]
    kv_w = k.shape[1]
    sgu_w = su.shape[1]
    rows = seqs * t_new
    crow = seqs * cache_len
    blk = lambda i: (i, 0)
    kern = functools.partial(_mix_out_sample_kernel, n_heads=n_heads, group=group, sgu_heads=sgu_heads, seqs=seqs,
                             t_new=t_new, cache_len=cache_len)
    return pl.pallas_call(
        kern,
        grid=(n // rows,),
        in_specs=[
            pl.BlockSpec(memory_space=pltpu.SMEM),
            pl.BlockSpec((rows, att_w), blk),
            pl.BlockSpec((rows, kv_w), blk),
            pl.BlockSpec((rows, kv_w), blk),
            pl.BlockSpec((crow, kv_w), blk),
            pl.BlockSpec((crow, kv_w), blk),
            pl.BlockSpec((rows, sgu_w), blk),
            pl.BlockSpec((rows, sgu_w), blk),
            pl.BlockSpec((rows, d), blk),
            _const_spec(wbig_bf.shape),
            _const_spec(bt.shape),
            _const_spec((1, att_w)),
            _const_spec((1, sgu_w)),
            _const_spec(w_out_bf.shape),
            _const_spec((1, d)),
        ],
        out_specs=[pl.BlockSpec((rows, d), blk), pl.BlockSpec((rows, d), blk),
                   pl.BlockSpec((crow, kv_w), blk), pl.BlockSpec((crow, kv_w), blk)],
        out_shape=[jax.ShapeDtypeStruct((n, d), F32), jax.ShapeDtypeStruct((n, d), BF16),
                   jax.ShapeDtypeStruct(ck.shape, F32), jax.ShapeDtypeStruct(cv.shape, F32)],
        compiler_params=pltpu.CompilerParams(dimension_semantics=("arbitrary",), vmem_limit_bytes=VMEM_LIMIT),
        name="mix_out_sample",
    )(sinks, q, k, v, ck, cv, su, sv, x2, wbig_bf, bt, aon, son, w_out_bf, nffn)


def _top16_rows(s, tie_key, big):
    t = s.shape[1]
    slot = lax.broadcasted_iota(jnp.int32, (PEER_TOPK, t), 0)

    def body(it, carry):
        s, vals, keys = carry
        m = jnp.max(s, axis=0, keepdims=True)
        key = jnp.min(jnp.where(s == m, tie_key, big), axis=0, keepdims=True)
        vals = jnp.where(slot == it, m, vals)
        keys = jnp.where(slot == it, key, keys)
        s = jnp.where(tie_key == key, NEG_INF, s)
        return s, vals, keys

    init = (s, jnp.zeros((PEER_TOPK, t), F32), jnp.zeros((PEER_TOPK, t), jnp.int32))
    _, vals, keys = lax.fori_loop(0, PEER_TOPK, body, init)
    return vals, keys


def _peer_route_kernel(hn_ref, wq_ref, k1_ref, k2_ref, ids_ref, g_ref):
    tm = hn_ref.shape[0]
    q = _dot(hn_ref[...], wq_ref[...]).astype(BF16)
    k1 = k1_ref[...]
    k2 = k2_ref[...]
    key_iota = lax.broadcasted_iota(jnp.int32, (N_KEYS, tm), 0)
    ids_rows = []
    g_rows = []
    for h in range(PEER_HEADS):
        base = h * 2 * PEER_HALF
        s1 = _dot_nt(k1, q[:, base:base + PEER_HALF])
        s2 = _dot_nt(k2, q[:, base + PEER_HALF:base + 2 * PEER_HALF])
        v1, i1 = _top16_rows(s1, key_iota, N_KEYS)
        v2, i2 = _top16_rows(s2, key_iota, N_KEYS)
        cand = jnp.concatenate([v1[a:a + 1, :] + v2 for a in range(PEER_TOPK)], axis=0)
        eid = jnp.concatenate([i1[a:a + 1, :] * N_KEYS + i2 for a in range(PEER_TOPK)], axis=0)
        sc, e = _top16_rows(cand, eid, N_KEYS * N_KEYS)
        ex = jnp.exp(sc - jnp.max(sc, axis=0, keepdims=True))
        g_rows.append(ex / jnp.sum(ex, axis=0, keepdims=True))
        ids_rows.append(e)
    ids_t = jnp.concatenate(ids_rows, axis=0)
    g_t = jnp.concatenate(g_rows, axis=0)
    ids_ref[...] = ids_t.T
    g_ref[...] = g_t.T


def _peer_route(hn, wq_bf, k1_bf, k2_bf, *, tm):
    n, d = hn.shape
    slots = PEER_HEADS * PEER_TOPK
    row = lambda i: (i, 0)
    return pl.pallas_call(
        _peer_route_kernel,
        grid=(n // tm,),
        in_specs=[pl.BlockSpec((tm, d), row), _const_spec(wq_bf.shape), _const_spec(k1_bf.shape),
                  _const_spec(k2_bf.shape)],
        out_specs=[pl.BlockSpec((tm, slots), row), pl.BlockSpec((tm, slots), row)],
        out_shape=[jax.ShapeDtypeStruct((n, slots), jnp.int32), jax.ShapeDtypeStruct((n, slots), F32)],
        compiler_params=pltpu.CompilerParams(dimension_semantics=("arbitrary",), vmem_limit_bytes=VMEM_LIMIT),
        name="peer_route",
    )(hn, wq_bf, k1_bf, k2_bf)


def _peer_apply_kernel(ids_ref, h_ref, hn_ref, g_ref, tab_ref, out_ref, buf, xf, sem, *, tb):
    slots = PEER_HEADS * PEER_TOPK
    rows = tb * slots
    step = pl.program_id(0)
    n_blocks = pl.num_programs(0) - 1

    def row_copy(slot, i, e):
        return pltpu.make_async_copy(tab_ref.at[pl.ds(e, 1), :], buf.at[slot, pl.ds(i, 1), :], sem.at[slot])

    @pl.when(step < n_blocks)
    def _issue():
        slot = step % 2

        def body(i, carry):
            row_copy(slot, i, ids_ref[i]).start()
            return carry

        lax.fori_loop(0, rows, body, 0, unroll=8)

    @pl.when(step >= 1)
    def _compute():
        slot = (step - 1) % 2
        pltpu.make_async_copy(tab_ref.at[pl.ds(0, rows), :], buf.at[slot], sem.at[slot]).wait()
        xf[...] = hn_ref[...].astype(F32)

        def body(t, carry):
            w = buf[slot, pl.ds(pl.multiple_of(t * slots, slots), slots), :]
            ub = pltpu.bitcast(w & jnp.uint32(0xFFFF0000), F32).astype(BF16)
            vb = pltpu.bitcast(w << 16, F32).astype(BF16)
            x8 = jnp.broadcast_to(xf[pl.ds(t, 1), :], (8, xf.shape[1])).astype(BF16)
            act = _gelu(_dot_nt(x8, ub))
            a = (g_ref[pl.ds(t, 1), :] * act).astype(BF16)
            o = _dot(a, vb)
            out_ref[pl.ds(t, 1), :] = h_ref[pl.ds(t, 1), :] + o[0:1, :]
            return carry

        lax.fori_loop(0, tb, body, 0)


def _peer_apply(ids_flat, h, hn, g, table, *, tb):
    n, d = h.shape
    slots = PEER_HEADS * PEER_TOPK
    n_blocks = n // tb
    last = n_blocks - 1
    issue_blk = lambda s: (jnp.minimum(s, last),)
    comp_blk = lambda s: (jnp.maximum(s - 1, 0), 0)
    kern = functools.partial(_peer_apply_kernel, tb=tb)
    return pl.pallas_call(
        kern,
        grid=(n_blocks + 1,),
        in_specs=[
            pl.BlockSpec((tb * slots,), issue_blk, memory_space=pltpu.SMEM),
            pl.BlockSpec((tb, d), comp_blk),
            pl.BlockSpec((tb, d), comp_blk),
            pl.BlockSpec((tb, slots), comp_blk),
            pl.BlockSpec(memory_space=pl.ANY),
        ],
        out_specs=pl.BlockSpec((tb, d), comp_blk),
        out_shape=jax.ShapeDtypeStruct((n, d), F32),
        scratch_shapes=[
            pltpu.VMEM((2, tb * slots, d), jnp.uint32),
            pltpu.VMEM((tb, d), F32),
            pltpu.SemaphoreType.DMA((2,)),
        ],
        compiler_params=pltpu.CompilerParams(dimension_semantics=("arbitrary",), vmem_limit_bytes=VMEM_LIMIT),
        name="peer_apply",
    )(ids_flat, h, hn, g, table)


def _ple_kernel(h_ref, p_ref, gn_ref, gw_ref, pw_ref, out_ref):
    h = h_ref[...]
    gate = jax.nn.sigmoid(_dot(_rms(h, gn_ref[...]).astype(BF16), gw_ref[...]))
    out_ref[...] = h + _dot(p_ref[...].astype(BF16), pw_ref[...]) * gate


def _ple(h, p, gn, gw_bf, pw_bf, *, tm):
    n, d = h.shape
    pd = p.shape[1]
    row = lambda i: (i, 0)
    return pl.pallas_call(
        _ple_kernel,
        grid=(n // tm,),
        in_specs=[pl.BlockSpec((tm, d), row), pl.BlockSpec((tm, pd), row), _const_spec((1, d)),
                  _const_spec(gw_bf.shape), _const_spec(pw_bf.shape)],
        out_specs=pl.BlockSpec((tm, d), row),
        out_shape=jax.ShapeDtypeStruct((n, d), F32),
        compiler_params=pltpu.CompilerParams(dimension_semantics=("arbitrary",), vmem_limit_bytes=VMEM_LIMIT),
        name="ple",
    )(h, p, gn, gw_bf, pw_bf)


def _rope_tables(pos):
    half = ROT_DIM // 2
    inv = jnp.power(jnp.float32(ROPE_THETA), -jnp.arange(half, dtype=F32) * 2.0 / ROT_DIM)
    ang = pos[:, None] * inv[None, :]
    cos = jnp.cos(ang)
    sin = jnp.sin(ang)
    n = pos.shape[0]
    zeros = jnp.zeros((n, HEAD_DIM - ROT_DIM), F32)
    zh = jnp.zeros((n, half), F32)
    c = jnp.concatenate([cos, cos, jnp.ones((n, HEAD_DIM - ROT_DIM), F32)], axis=1)
    s1 = jnp.concatenate([-sin, zh, zeros], axis=1)
    s2 = jnp.concatenate([zh, sin, zeros], axis=1)
    two = lambda a: jnp.concatenate([a, a], axis=1)
    return two(c), two(s1), two(s2)


def _head_indicator(width):
    head = jnp.arange(width) // HEAD_DIM
    ind = (head[:, None] == jnp.arange(LANES)[None, :]).astype(BF16)
    return ind, ind.T


def _pack_tables(u_tab, v_tab):
    ub = lax.bitcast_convert_type(u_tab.astype(BF16), jnp.uint16).astype(jnp.uint32)
    vb = lax.bitcast_convert_type(v_tab.astype(BF16), jnp.uint16).astype(jnp.uint32)
    return (ub << 16) | vb


def _layer(x2, p2, lw, stream):
    (norm_mix, w_in_bf, gain_qk, ind, indt, sinks, sgu_norm, wm_full, sgu_b, aon, son, w_out_bf, nffn, wq_bf, k1_bf,
     k2_bf, table, pw_bf, gn, gw_bf, dims) = lw
    att_w, kv_w, sgu_w, n_heads, group, sgu_heads = dims
    n, d = x2.shape
    if stream["kind"] == "prompt":
        seq = stream["seq"]
        tm = TM_IN
        rope_blocks = seq // tm
    else:
        tm = stream["rows"]
        rope_blocks = 1
    q, k, v, su, sv = _mix_in(x2, norm_mix, w_in_bf, gain_qk, ind, indt, *stream["rope"], sgu_norm, tm=tm,
                              rope_blocks=rope_blocks, att_w=att_w, kv_w=kv_w, sgu_w=sgu_w)
    if stream["kind"] == "prompt":
        c = min(stream["seq"], CHUNK)
        tril = jnp.tril(jnp.ones((c, c), F32))
        wm_bf = (wm_full[:, :c, :c] * tril).astype(BF16)
        bt = sgu_b[:, :c].T
        h, hn = _mix_out_prompt(sinks, q, k, v, su, sv, x2, wm_bf, bt, aon, son, w_out_bf, nffn,
                                batch=stream["batch"], seq=stream["seq"], n_heads=n_heads, group=group,
                                sgu_heads=sgu_heads)
        k_state = k.reshape(stream["batch"], stream["seq"], kv_w)[:, -WINDOW:]
        v_state = v.reshape(stream["batch"], stream["seq"], kv_w)[:, -WINDOW:]
    else:
        t_new = stream["t_new"]
        seqs = stream["rows"] // t_new
        c = min(t_new, CHUNK)
        tril = jnp.tril(jnp.ones((c, c), F32))
        wm_c = wm_full[:, :c, :c] * tril
        eye = jnp.eye(seqs, dtype=F32)
        wbig_bf = jnp.einsum("ab,hts->hatbs", eye, wm_c).reshape(sgu_heads, seqs * c, seqs * c).astype(BF16)
        bt = jnp.tile(sgu_b[:, :c].T, (seqs, 1))
        ck, cv = stream["cache"]
        cache_len = ck.shape[0] // (n // t_new)
        h, hn, wk, wv = _mix_out_sample(sinks, q, k, v, ck, cv, su, sv, x2, wbig_bf, bt, aon, son, w_out_bf, nffn,
                                        n_heads=n_heads, group=group, sgu_heads=sgu_heads, seqs=seqs, t_new=t_new,
                                        cache_len=cache_len)
        k_state = wk.reshape(n // t_new, cache_len, kv_w)[:, -WINDOW:]
        v_state = wv.reshape(n // t_new, cache_len, kv_w)[:, -WINDOW:]
    ids, g = _peer_route(hn, wq_bf, k1_bf, k2_bf, tm=min(TM_IN, n))
    h2 = _peer_apply(ids.reshape(-1), h, hn, g, table, tb=PEER_TB)
    out = _ple(h2, p2, gn, gw_bf, pw_bf, tm=min(TM_IN, n))
    return out, k_state, v_state, sv


def kernel(x_prompt, x_sample, cache_k, cache_v, p_prompt, p_sample, norm_mix, w_in, q_norm, k_norm, sinks, sgu_norm,
           sgu_w, sgu_b, attn_out_norm, sgu_out_norm, w_out, norm_ffn, peer_wq, peer_k1, peer_k2, peer_u, peer_v,
           ple_w, ple_gate_norm, ple_gate_w):
    batch, seq, d = x_prompt.shape
    dec_batch, dec_seq, _ = x_sample.shape
    depth = w_in.shape[0]
    n_heads = sinks.shape[1]
    n_kv = cache_k.shape[3]
    group = n_heads // n_kv
    att_w = n_heads * HEAD_DIM
    kv_w = n_kv * HEAD_DIM
    sgu_heads = sgu_w.shape[1]
    sgu_width = sgu_out_norm.shape[1]
    cache_len = cache_k.shape[2]
    assert kv_w == LANES and seq % TM_IN == 0 and TM_IN % ATT_BLOCK == 0
    assert cache_len == WINDOW, "sample attention assumes a full window buffer"

    pos_p = jnp.arange(seq, dtype=F32)
    pos_s = PAST_LEN + jnp.arange(dec_seq, dtype=F32)
    rows_s = 16 * dec_seq
    rope_p = _rope_tables(pos_p)
    rope_s = tuple(jnp.tile(a, (rows_s // dec_seq, 1)) for a in _rope_tables(pos_s))
    ind, indt = _head_indicator(att_w + kv_w)

    hp = x_prompt.reshape(batch * seq, d)
    hs = x_sample.reshape(dec_batch * dec_seq, d)
    kp_l, vp_l, ks_l, vs_l, sv_l = [], [], [], [], []
    for i in range(depth):
        gain_qk = jnp.concatenate([jnp.tile(q_norm[i], n_heads), jnp.tile(k_norm[i], n_kv)])[None, :]
        lw = (norm_mix[i][None, :], w_in[i].astype(BF16), gain_qk, ind, indt, sinks[i], sgu_norm[i][None, :],
              sgu_w[i], sgu_b[i], attn_out_norm[i][None, :], sgu_out_norm[i][None, :], w_out[i].astype(BF16),
              norm_ffn[i][None, :], peer_wq[i].astype(BF16), peer_k1[i].astype(BF16), peer_k2[i].astype(BF16),
              _pack_tables(peer_u[i], peer_v[i]), ple_w[i].astype(BF16), ple_gate_norm[i][None, :],
              ple_gate_w[i].astype(BF16), (att_w, kv_w, sgu_width, n_heads, group, sgu_heads))
        prompt = dict(kind="prompt", batch=batch, seq=seq, rope=rope_p)
        sample = dict(kind="sample", rows=rows_s, t_new=dec_seq, rope=rope_s,
                      cache=(cache_k[i].reshape(dec_batch * cache_len, kv_w),
                             cache_v[i].reshape(dec_batch * cache_len, kv_w)))
        hp, kp, vp, _ = _layer(hp, p_prompt[i].reshape(batch * seq, -1), lw, prompt)
        hs, kss, vss, svs = _layer(hs, p_sample[i].reshape(dec_batch * dec_seq, -1), lw, sample)
        kp_l.append(kp.reshape(batch, WINDOW, n_kv, HEAD_DIM))
        vp_l.append(vp.reshape(batch, WINDOW, n_kv, HEAD_DIM))
        ks_l.append(kss.reshape(dec_batch, WINDOW, n_kv, HEAD_DIM))
        vs_l.append(vss.reshape(dec_batch, WINDOW, n_kv, HEAD_DIM))
        sv_l.append(svs.reshape(dec_batch, dec_seq, sgu_heads, -1))
    return (hp.reshape(batch, seq, d), hs.reshape(dec_batch, dec_seq, d), jnp.stack(kp_l), jnp.stack(vp_l),
            jnp.stack(ks_l), jnp.stack(vs_l), jnp.stack(sv_l))
```

```python
import functools

import jax
import jax.numpy as jnp
from jax import lax
from jax.experimental import pallas as pl
from jax.experimental.pallas import tpu as pltpu

F32 = jnp.float32
BF16 = jnp.bfloat16

EPS = 1e-6
PAST_LEN = 16384
ROPE_THETA = 500000.0
HEAD_DIM = 64
ROT_DIM = HEAD_DIM // 4
WINDOW = 128
CHUNK = 128
N_KEYS = 128
PEER_HEADS = 8
PEER_TOPK = 16
PEER_HALF = 128

LANES = 128
SUBLANES = 8
VMEM_LIMIT = 56 * 1024 * 1024

TM_IN = 256
ATT_BLOCK = 128
PEER_TB = 16
NEG_INF = float("-inf")


def _gelu(x):
    return 0.5 * x * (1.0 + jnp.tanh(0.7978845608028654 * (x + 0.044715 * (x * x * x))))


def _rms(x, gain):
    return x * lax.rsqrt(jnp.mean(x * x, axis=-1, keepdims=True) + EPS) * gain


def _dot(a, b):
    return jnp.dot(a, b, preferred_element_type=F32)


def _dot_nt(a, b):
    return lax.dot_general(a, b, (((1,), (1,)), ((), ())), preferred_element_type=F32)


def _dot_split(a, b01):
    hi = a.astype(BF16)
    lo = (a - hi.astype(F32)).astype(BF16)
    return _dot(hi, b01) + _dot(lo, b01)


def _const_spec(shape):
    nd = len(shape)
    return pl.BlockSpec(shape, lambda *_: (0,) * nd)


def _mix_in_kernel(x_ref, nm_ref, w_ref, gain_ref, ind_ref, indt_ref, c_ref, s1_ref, s2_ref, sn_ref,
                   q_ref, k_ref, v_ref, su_ref, sv_ref, *, att_w, kv_w, sgu_w):
    x = x_ref[...]
    hn = _rms(x, nm_ref[...])
    proj = _dot(hn.astype(BF16), w_ref[...])
    qk_w = att_w + kv_w
    o3 = qk_w + kv_w
    o4 = o3 + sgu_w
    qk = proj[:, :qk_w]
    ssq = _dot_split(qk * qk, ind_ref[...])
    inv = lax.rsqrt(ssq * (1.0 / HEAD_DIM) + EPS)
    y = qk * _dot_split(inv, indt_ref[...]) * gain_ref[...]
    c = c_ref[...]
    s1 = s1_ref[...]
    s2 = s2_ref[...]
    parts = []
    for j in range(qk_w // LANES):
        yj = y[:, j * LANES:(j + 1) * LANES]
        parts.append(yj * c + pltpu.roll(yj, LANES - ROT_DIM // 2, 1) * s1 + pltpu.roll(yj, ROT_DIM // 2, 1) * s2)
    for j in range(att_w // LANES):
        q_ref[:, j * LANES:(j + 1) * LANES] = parts[j].astype(q_ref.dtype)
    for j in range(kv_w // LANES):
        k_ref[:, j * LANES:(j + 1) * LANES] = parts[att_w // LANES + j]
    v_ref[...] = proj[:, qk_w:o3]
    su_ref[...] = _gelu(proj[:, o3:o4])
    sn = sn_ref[...]
    for h in range(sgu_w // LANES):
        blk = _gelu(proj[:, o4 + h * LANES:o4 + (h + 1) * LANES])
        sv_ref[:, h * LANES:(h + 1) * LANES] = _rms(blk, sn)


def _mix_in(x2, norm_mix, w_in_bf, gain_qk, ind, indt, rope_c, rope_s1, rope_s2, sgu_norm, *, tm, rope_blocks,
            att_w, kv_w, sgu_w):
    n, d = x2.shape
    in_dim = w_in_bf.shape[1]
    qk_w = att_w + kv_w
    grid = (n // tm,)
    row = lambda i: (i, 0)
    rope_map = (lambda i: (i % rope_blocks, 0)) if rope_blocks > 1 else (lambda i: (0, 0))
    kern = functools.partial(_mix_in_kernel, att_w=att_w, kv_w=kv_w, sgu_w=sgu_w)
    return pl.pallas_call(
        kern,
        grid=grid,
        in_specs=[
            pl.BlockSpec((tm, d), row),
            _const_spec((1, d)),
            _const_spec((d, in_dim)),
            _const_spec((1, qk_w)),
            _const_spec((qk_w, LANES)),
            _const_spec((LANES, qk_w)),
            pl.BlockSpec((tm, LANES), rope_map),
            pl.BlockSpec((tm, LANES), rope_map),
            pl.BlockSpec((tm, LANES), rope_map),
            _const_spec((1, LANES)),
        ],
        out_specs=[
            pl.BlockSpec((tm, att_w), row),
            pl.BlockSpec((tm, kv_w), row),
            pl.BlockSpec((tm, kv_w), row),
            pl.BlockSpec((tm, sgu_w), row),
            pl.BlockSpec((tm, sgu_w), row),
        ],
        out_shape=[
            jax.ShapeDtypeStruct((n, att_w), BF16),
            jax.ShapeDtypeStruct((n, kv_w), F32),
            jax.ShapeDtypeStruct((n, kv_w), F32),
            jax.ShapeDtypeStruct((n, sgu_w), F32),
            jax.ShapeDtypeStruct((n, sgu_w), F32),
        ],
        compiler_params=pltpu.CompilerParams(dimension_semantics=("arbitrary",), vmem_limit_bytes=VMEM_LIMIT),
        name="mix_in",
    )(x2, norm_mix, w_in_bf, gain_qk, ind, indt, rope_c, rope_s1, rope_s2, sgu_norm)


def _dup_heads(kv2):
    lane = lax.broadcasted_iota(jnp.int32, kv2.shape, 1)
    rolled = pltpu.roll(kv2, HEAD_DIM, 1)
    lo = lane < HEAD_DIM
    return jnp.where(lo, kv2, rolled), jnp.where(lo, rolled, kv2)


def _sink_attention(q_ref, kdup, vdup, valid, sinks_ref, n_heads, group):
    rows = valid.shape[0]
    lane = lax.broadcasted_iota(jnp.int32, (rows, LANES), 1)
    lo = lane < HEAD_DIM
    zero = jnp.zeros((rows, LANES), BF16)
    outs = []
    for pair in range(n_heads // 2):
        kv = (2 * pair) // group
        q_pair = q_ref[:, pair * LANES:(pair + 1) * LANES]
        halves = []
        for half in range(2):
            head = 2 * pair + half
            qm = jnp.where(lo if half == 0 else jnp.logical_not(lo), q_pair, zero)
            s = _dot_nt(qm, kdup[kv]) * (HEAD_DIM ** -0.5)
            s = jnp.where(valid, s, NEG_INF)
            sink = sinks_ref[head]
            m = jnp.maximum(jnp.max(s, axis=-1, keepdims=True), sink)
            e = jnp.exp(s - m)
            denom = jnp.sum(e, axis=-1, keepdims=True) + jnp.exp(sink - m)
            p = (e / denom).astype(BF16)
            halves.append(_dot(p, vdup[kv]))
        outs.append(jnp.where(lo, halves[0], halves[1]))
    return jnp.concatenate(outs, axis=1)


def _sgu(su_ref, sv_ref, wm_ref, bt_ref, n_heads):
    outs = []
    for h in range(n_heads):
        sl = slice(h * LANES, (h + 1) * LANES)
        z = _dot(wm_ref[h], sv_ref[:, sl].astype(BF16)) + bt_ref[:, h:h + 1]
        outs.append(su_ref[:, sl] * z)
    return jnp.concatenate(outs, axis=1)


def _mix_tail(attn, sg, x_ref, aon_ref, son_ref, wout_ref, nffn_ref, h_ref, hn_ref):
    mixed = jnp.concatenate([_rms(attn, aon_ref[...]), _rms(sg, son_ref[...])], axis=1).astype(BF16)
    h = x_ref[...] + _dot(mixed, wout_ref[...])
    h_ref[...] = h
    hn_ref[...] = _rms(h, nffn_ref[...]).astype(hn_ref.dtype)


def _mix_out_prompt_kernel(sinks_ref, q_ref, kc_ref, kp_ref, vc_ref, vp_ref, su_ref, sv_ref, x_ref, wm_ref, bt_ref,
                           aon_ref, son_ref, wout_ref, nffn_ref, h_ref, hn_ref, *, n_heads, group, sgu_heads):
    j = pl.program_id(1)
    k2 = jnp.concatenate([kp_ref[...], kc_ref[...]], axis=0).astype(BF16)
    v2 = jnp.concatenate([vp_ref[...], vc_ref[...]], axis=0).astype(BF16)
    kdup = _dup_heads(k2)
    vdup = _dup_heads(v2)
    row = lax.broadcasted_iota(jnp.int32, (ATT_BLOCK, 2 * ATT_BLOCK), 0)
    col = lax.broadcasted_iota(jnp.int32, (ATT_BLOCK, 2 * ATT_BLOCK), 1)
    in_prev = col < ATT_BLOCK
    valid = jnp.logical_or(jnp.logical_and(jnp.logical_and(in_prev, col > row), j > 0),
                           jnp.logical_and(jnp.logical_not(in_prev), col - ATT_BLOCK <= row))
    attn = _sink_attention(q_ref, kdup, vdup, valid, sinks_ref, n_heads, group)
    sg = _sgu(su_ref, sv_ref, wm_ref, bt_ref, sgu_heads)
    _mix_tail(attn, sg, x_ref, aon_ref, son_ref, wout_ref, nffn_ref, h_ref, hn_ref)


def _mix_out_prompt(sinks, q, k, v, su, sv, x2, wm_bf, bt, aon, son, w_out_bf, nffn, *, batch, seq, n_heads, group,
                    sgu_heads):
    n, d = x2.shape
    att_w = q.shape[1]
    kv_w = k.shape[1]
    sgu_w = su.shape[1]
    nb = seq // ATT_BLOCK
    cur = lambda b, j: (b * nb + j, 0)
    prev = lambda b, j: (b * nb + jnp.maximum(j - 1, 0), 0)
    kern = functools.partial(_mix_out_prompt_kernel, n_heads=n_heads, group=group, sgu_heads=sgu_heads)
    return pl.pallas_call(
        kern,
        grid=(batch, nb),
        in_specs=[
            pl.BlockSpec(memory_space=pltpu.SMEM),
            pl.BlockSpec((ATT_BLOCK, att_w), cur),
            pl.BlockSpec((ATT_BLOCK, kv_w), cur),
            pl.BlockSpec((ATT_BLOCK, kv_w), prev),
            pl.BlockSpec((ATT_BLOCK, kv_w), cur),
            pl.BlockSpec((ATT_BLOCK, kv_w), prev),
            pl.BlockSpec((ATT_BLOCK, sgu_w), cur),
            pl.BlockSpec((ATT_BLOCK, sgu_w), cur),
            pl.BlockSpec((ATT_BLOCK, d), cur),
            _const_spec(wm_bf.shape),
            _const_spec(bt.shape),
            _const_spec((1, att_w)),
            _const_spec((1, sgu_w)),
            _const_spec(w_out_bf.shape),
            _const_spec((1, d)),
        ],
        out_specs=[pl.BlockSpec((ATT_BLOCK, d), cur), pl.BlockSpec((ATT_BLOCK, d), cur)],
        out_shape=[jax.ShapeDtypeStruct((n, d), F32), jax.ShapeDtypeStruct((n, d), BF16)],
        compiler_params=pltpu.CompilerParams(dimension_semantics=("arbitrary", "arbitrary"),
                                             vmem_limit_bytes=VMEM_LIMIT),
        name="mix_out_prompt",
    )(sinks, q, k, k, v, v, su, sv, x2, wm_bf, bt, aon, son, w_out_bf, nffn)


def _mix_out_sample_kernel(sinks_ref, q_ref, kn_ref, vn_ref, ck_ref, cv_ref, su_ref, sv_ref, x_ref, wm_ref, bt_ref,
                           aon_ref, son_ref, wout_ref, nffn_ref, h_ref, hn_ref, wk_ref, wv_ref, *, n_heads, group,
                           sgu_heads, seqs, t_new, cache_len):
    rows = seqs * t_new
    n_cache = seqs * cache_len
    k_all = jnp.concatenate([ck_ref[...], kn_ref[...]], axis=0).astype(BF16)
    v_all = jnp.concatenate([cv_ref[...], vn_ref[...]], axis=0).astype(BF16)
    kdup = _dup_heads(k_all)
    vdup = _dup_heads(v_all)
    shape = (rows, n_cache + rows)
    row = lax.broadcasted_iota(jnp.int32, shape, 0)
    col = lax.broadcasted_iota(jnp.int32, shape, 1)
    r_seq = row // t_new
    r_pos = row % t_new
    in_cache = col < n_cache
    cn = col - n_cache
    c_seq = jnp.where(in_cache, col // cache_len, cn // t_new)
    ok = jnp.logical_or(jnp.logical_and(in_cache, (col % cache_len) > r_pos + (cache_len - WINDOW)),
                        jnp.logical_and(jnp.logical_not(in_cache), (cn % t_new) <= r_pos))
    valid = jnp.logical_and(c_seq == r_seq, ok)
    attn = _sink_attention(q_ref, kdup, vdup, valid, sinks_ref, n_heads, group)
    sg = _sgu(su_ref, sv_ref, wm_ref, bt_ref, sgu_heads)
    _mix_tail(attn, sg, x_ref, aon_ref, son_ref, wout_ref, nffn_ref, h_ref, hn_ref)
    keep = cache_len - t_new
    for s in range(seqs):
        base = s * cache_len
        wk_ref[base:base + keep, :] = ck_ref[base + t_new:base + cache_len, :]
        wk_ref[base + keep:base + cache_len, :] = kn_ref[s * t_new:(s + 1) * t_new, :]
        wv_ref[base:base + keep, :] = cv_ref[base + t_new:base + cache_len, :]
        wv_ref[base + keep:base + cache_len, :] = vn_ref[s * t_new:(s + 1) * t_new, :]


def _mix_out_sample(sinks, q, k, v, ck, cv, su, sv, x2, wbig_bf, bt, aon, son, w_out_bf, nffn, *, n_heads, group,
                    sgu_heads, seqs, t_new, cache_len):
    n, d = x2.shape
    att_w = q.shape[1]
    kv_w = k.shape[1]
    sgu_w = su.shape[1]
    rows = seqs * t_new
    crow = seqs * cache_len
    blk = lambda i: (i, 0)
    kern = functools.partial(_mix_out_sample_kernel, n_heads=n_heads, group=group, sgu_heads=sgu_heads, seqs=seqs,
                             t_new=t_new, cache_len=cache_len)
    return pl.pallas_call(
        kern,
        grid=(n // rows,),
        in_specs=[
            pl.BlockSpec(memory_space=pltpu.SMEM),
            pl.BlockSpec((rows, att_w), blk),
            pl.BlockSpec((rows, kv_w), blk),
            pl.BlockSpec((rows, kv_w), blk),
            pl.BlockSpec((crow, kv_w), blk),
            pl.BlockSpec((crow, kv_w), blk),
            pl.BlockSpec((rows, sgu_w), blk),
            pl.BlockSpec((rows, sgu_w), blk),
            pl.BlockSpec((rows, d), blk),
            _const_spec(wbig_bf.shape),
            _const_spec(bt.shape),
            _const_spec((1, att_w)),
            _const_spec((1, sgu_w)),
            _const_spec(w_out_bf.shape),
            _const_spec((1, d)),
        ],
        out_specs=[pl.BlockSpec((rows, d), blk), pl.BlockSpec((rows, d), blk),
                   pl.BlockSpec((crow, kv_w), blk), pl.BlockSpec((crow, kv_w), blk)],
        out_shape=[jax.ShapeDtypeStruct((n, d), F32), jax.ShapeDtypeStruct((n, d), BF16),
                   jax.ShapeDtypeStruct(ck.shape, F32), jax.ShapeDtypeStruct(cv.shape, F32)],
        compiler_params=pltpu.CompilerParams(dimension_semantics=("arbitrary",), vmem_limit_bytes=VMEM_LIMIT),
        name="mix_out_sample",
    )(sinks, q, k, v, ck, cv, su, sv, x2, wbig_bf, bt, aon, son, w_out_bf, nffn)


def _top16_rows(s, tie_key, big):
    t = s.shape[1]
    slot = lax.broadcasted_iota(jnp.int32, (PEER_TOPK, t), 0)

    def body(it, carry):
        s, vals, keys = carry
        m = jnp.max(s, axis=0, keepdims=True)
        key = jnp.min(jnp.where(s == m, tie_key, big), axis=0, keepdims=True)
        vals = jnp.where(slot == it, m, vals)
        keys = jnp.where(slot == it, key, keys)
        s = jnp.where(tie_key == key, NEG_INF, s)
        return s, vals, keys

    init = (s, jnp.zeros((PEER_TOPK, t), F32), jnp.zeros((PEER_TOPK, t), jnp.int32))
    _, vals, keys = lax.fori_loop(0, PEER_TOPK, body, init)
    return vals, keys


def _peer_route_kernel(hn_ref, wq_ref, k1_ref, k2_ref, ids_ref, g_ref):
    tm = hn_ref.shape[0]
    q = _dot(hn_ref[...], wq_ref[...]).astype(BF16)
    k1 = k1_ref[...]
    k2 = k2_ref[...]
    key_iota = lax.broadcasted_iota(jnp.int32, (N_KEYS, tm), 0)
    ids_rows = []
    g_rows = []
    for h in range(PEER_HEADS):
        base = h * 2 * PEER_HALF
        s1 = _dot_nt(k1, q[:, base:base + PEER_HALF])
        s2 = _dot_nt(k2, q[:, base + PEER_HALF:base + 2 * PEER_HALF])
        v1, i1 = _top16_rows(s1, key_iota, N_KEYS)
        v2, i2 = _top16_rows(s2, key_iota, N_KEYS)
        cand = jnp.concatenate([v1[a:a + 1, :] + v2 for a in range(PEER_TOPK)], axis=0)
        eid = jnp.concatenate([i1[a:a + 1, :] * N_KEYS + i2 for a in range(PEER_TOPK)], axis=0)
        sc, e = _top16_rows(cand, eid, N_KEYS * N_KEYS)
        ex = jnp.exp(sc - jnp.max(sc, axis=0, keepdims=True))
        g_rows.append(ex / jnp.sum(ex, axis=0, keepdims=True))
        ids_rows.append(e)
    ids_t = jnp.concatenate(ids_rows, axis=0)
    g_t = jnp.concatenate(g_rows, axis=0)
    ids_ref[...] = ids_t.T
    g_ref[...] = g_t.T


def _peer_route(hn, wq_bf, k1_bf, k2_bf, *, tm):
    n, d = hn.shape
    slots = PEER_HEADS * PEER_TOPK
    row = lambda i: (i, 0)
    return pl.pallas_call(
        _peer_route_kernel,
        grid=(n // tm,),
        in_specs=[pl.BlockSpec((tm, d), row), _const_spec(wq_bf.shape), _const_spec(k1_bf.shape),
                  _const_spec(k2_bf.shape)],
        out_specs=[pl.BlockSpec((tm, slots), row), pl.BlockSpec((tm, slots), row)],
        out_shape=[jax.ShapeDtypeStruct((n, slots), jnp.int32), jax.ShapeDtypeStruct((n, slots), F32)],
        compiler_params=pltpu.CompilerParams(dimension_semantics=("arbitrary",), vmem_limit_bytes=VMEM_LIMIT),
        name="peer_route",
    )(hn, wq_bf, k1_bf, k2_bf)


def _peer_apply_kernel(ids_ref, h_ref, hn_ref, g_ref, tab_ref, out_ref, buf, xf, sem, *, tb):
    slots = PEER_HEADS * PEER_TOPK
    tiles = slots // SUBLANES
    chunks = xf.shape[1] // LANES
    per_chunk = slots // chunks
    step = pl.program_id(0)
    n_blocks = pl.num_programs(0) - 1
    islot = step % 2
    cslot = 1 - islot

    def issue(t, lo, hi):
        for j in range(lo, hi):
            e = ids_ref[t * slots + j]
            dst = buf.at[islot, t * tiles + j // SUBLANES, :, pl.ds(j % SUBLANES, 1), :]
            pltpu.make_async_copy(tab_ref.at[e], dst, sem.at[islot, t]).start(priority=j % 2)

    def tile(t, r, c):
        return buf[cslot, t * tiles + r, c]

    def iteration(t, carry, dot_u, act_gate, dot_v, do_issue):
        acc_prev, a_prev = carry
        if dot_u:
            rows = buf.at[cslot, pl.ds(t * tiles, tiles)]
            pltpu.make_async_copy(rows, rows, sem.at[cslot, t]).wait()
            xrow = xf[pl.ds(t, 1), :]
        a_new = a_prev
        if act_gate:
            hi = acc_prev.astype(BF16)
            lo = (acc_prev - hi.astype(F32)).astype(BF16)
            ones = jnp.ones((SUBLANES, LANES), BF16)
            act = _dot_nt(ones, hi) + _dot_nt(ones, lo)
            a_new = g_ref[pl.ds(t - 1, 1), :] * _gelu(act)
        acc = [jnp.zeros((SUBLANES, LANES), F32) for _ in range(tiles)]
        outs = []
        a_bf = a_prev.astype(BF16)
        tiles_per_start = tiles // per_chunk

        def v_slab(rs, c):
            return jnp.concatenate([pltpu.bitcast(tile(t - 2, r, c) << 16, F32) for r in rs], axis=0)

        for c in range(chunks):
            if dot_u:
                xb = jnp.broadcast_to(xrow[:, c * LANES:(c + 1) * LANES], (SUBLANES, LANES))
            vparts = []
            v_now = dot_v and 2 * c < chunks
            for i in range(per_chunk):
                if do_issue:
                    issue(t, c * per_chunk + i, c * per_chunk + i + 1)
                rs = range(i * tiles_per_start, (i + 1) * tiles_per_start)
                if dot_u:
                    for r in rs:
                        acc[r] = acc[r] + pltpu.bitcast(tile(t, r, c) & jnp.uint32(0xFFFF0000), F32) * xb
                if v_now:
                    vparts.append(jnp.concatenate([v_slab(rs, 2 * c), v_slab(rs, 2 * c + 1)], axis=1).astype(BF16))
            if v_now:
                outs.append(_dot(a_bf, jnp.concatenate(vparts, axis=0))[0:1, :])
        if dot_v:
            out_ref[pl.ds(t - 2, 1), :] = h_ref[pl.ds(t - 2, 1), :] + jnp.concatenate(outs, axis=1)
        return jnp.concatenate(acc, axis=0), a_new

    def compute(do_issue):
        xf[...] = hn_ref[...].astype(F32)
        carry = (jnp.zeros((slots, LANES), F32), jnp.zeros((SUBLANES, slots), F32))
        carry = iteration(0, carry, True, False, False, do_issue)
        carry = iteration(1, carry, True, True, False, do_issue)
        carry = lax.fori_loop(2, tb, lambda t, cr: iteration(t, cr, True, True, True, do_issue), carry)
        carry = iteration(tb, carry, False, True, True, False)
        iteration(tb + 1, carry, False, False, True, False)

    @pl.when(step == 0)
    def _first():
        def body(t, carry):
            issue(t, 0, slots)
            return carry

        lax.fori_loop(0, tb, body, 0)

    @pl.when(jnp.logical_and(step > 0, step < n_blocks))
    def _steady():
        compute(True)

    @pl.when(step == n_blocks)
    def _last():
        compute(False)


def _peer_apply(ids_flat, h, hn, g, table, *, tb):
    n, d = h.shape
    slots = PEER_HEADS * PEER_TOPK
    n_blocks = n // tb
    last = n_blocks - 1
    issue_blk = lambda s: (jnp.minimum(s, last),)
    comp_blk = lambda s: (jnp.maximum(s - 1, 0), 0)
    kern = functools.partial(_peer_apply_kernel, tb=tb)
    return pl.pallas_call(
        kern,
        grid=(n_blocks + 1,),
        in_specs=[
            pl.BlockSpec((tb * slots,), issue_blk, memory_space=pltpu.SMEM),
            pl.BlockSpec((tb, d), comp_blk),
            pl.BlockSpec((tb, d), comp_blk),
            pl.BlockSpec((tb, slots), comp_blk),
            pl.BlockSpec(memory_space=pl.ANY),
        ],
        out_specs=pl.BlockSpec((tb, d), comp_blk),
        out_shape=jax.ShapeDtypeStruct((n, d), F32),
        scratch_shapes=[
            pltpu.VMEM((2, tb * slots // SUBLANES, d // LANES, SUBLANES, LANES), jnp.uint32),
            pltpu.VMEM((tb, d), F32),
            pltpu.SemaphoreType.DMA((2, tb)),
        ],
        compiler_params=pltpu.CompilerParams(dimension_semantics=("arbitrary",), vmem_limit_bytes=VMEM_LIMIT),
        name="peer_apply",
    )(ids_flat, h, hn, g, table)


def _ple_kernel(h_ref, p_ref, gn_ref, gw_ref, pw_ref, out_ref):
    h = h_ref[...]
    gate = jax.nn.sigmoid(_dot(_rms(h, gn_ref[...]).astype(BF16), gw_ref[...]))
    out_ref[...] = h + _dot(p_ref[...].astype(BF16), pw_ref[...]) * gate


def _ple(h, p, gn, gw_bf, pw_bf, *, tm):
    n, d = h.shape
    pd = p.shape[1]
    row = lambda i: (i, 0)
    return pl.pallas_call(
        _ple_kernel,
        grid=(n // tm,),
        in_specs=[pl.BlockSpec((tm, d), row), pl.BlockSpec((tm, pd), row), _const_spec((1, d)),
                  _const_spec(gw_bf.shape), _const_spec(pw_bf.shape)],
        out_specs=pl.BlockSpec((tm, d), row),
        out_shape=jax.ShapeDtypeStruct((n, d), F32),
        compiler_params=pltpu.CompilerParams(dimension_semantics=("arbitrary",), vmem_limit_bytes=VMEM_LIMIT),
        name="ple",
    )(h, p, gn, gw_bf, pw_bf)


def _rope_tables(pos):
    half = ROT_DIM // 2
    inv = jnp.power(jnp.float32(ROPE_THETA), -jnp.arange(half, dtype=F32) * 2.0 / ROT_DIM)
    ang = pos[:, None] * inv[None, :]
    cos = jnp.cos(ang)
    sin = jnp.sin(ang)
    n = pos.shape[0]
    zeros = jnp.zeros((n, HEAD_DIM - ROT_DIM), F32)
    zh = jnp.zeros((n, half), F32)
    c = jnp.concatenate([cos, cos, jnp.ones((n, HEAD_DIM - ROT_DIM), F32)], axis=1)
    s1 = jnp.concatenate([-sin, zh, zeros], axis=1)
    s2 = jnp.concatenate([zh, sin, zeros], axis=1)
    two = lambda a: jnp.concatenate([a, a], axis=1)
    return two(c), two(s1), two(s2)


def _head_indicator(width):
    head = jnp.arange(width) // HEAD_DIM
    ind = (head[:, None] == jnp.arange(LANES)[None, :]).astype(BF16)
    return ind, ind.T


def _pack_tables(u_tab, v_tab):
    ub = lax.bitcast_convert_type(u_tab.astype(BF16), jnp.uint16).astype(jnp.uint32)
    vb = lax.bitcast_convert_type(v_tab.astype(BF16), jnp.uint16).astype(jnp.uint32)
    packed = (ub << 16) | vb
    return packed.reshape(packed.shape[0], packed.shape[1] // LANES, 1, LANES)


def _layer(x2, p2, lw, stream):
    (norm_mix, w_in_bf, gain_qk, ind, indt, sinks, sgu_norm, wm_full, sgu_b, aon, son, w_out_bf, nffn, wq_bf, k1_bf,
     k2_bf, table, pw_bf, gn, gw_bf, dims) = lw
    att_w, kv_w, sgu_w, n_heads, group, sgu_heads = dims
    n, d = x2.shape
    if stream["kind"] == "prompt":
        seq = stream["seq"]
        tm = TM_IN
        rope_blocks = seq // tm
    else:
        tm = stream["rows"]
        rope_blocks = 1
    q, k, v, su, sv = _mix_in(x2, norm_mix, w_in_bf, gain_qk, ind, indt, *stream["rope"], sgu_norm, tm=tm,
                              rope_blocks=rope_blocks, att_w=att_w, kv_w=kv_w, sgu_w=sgu_w)
    if stream["kind"] == "prompt":
        c = min(stream["seq"], CHUNK)
        tril = jnp.tril(jnp.ones((c, c), F32))
        wm_bf = (wm_full[:, :c, :c] * tril).astype(BF16)
        bt = sgu_b[:, :c].T
        h, hn = _mix_out_prompt(sinks, q, k, v, su, sv, x2, wm_bf, bt, aon, son, w_out_bf, nffn,
                                batch=stream["batch"], seq=stream["seq"], n_heads=n_heads, group=group,
                                sgu_heads=sgu_heads)
        k_state = k.reshape(stream["batch"], stream["seq"], kv_w)[:, -WINDOW:]
        v_state = v.reshape(stream["batch"], stream["seq"], kv_w)[:, -WINDOW:]
    else:
        t_new = stream["t_new"]
        seqs = stream["rows"] // t_new
        c = min(t_new, CHUNK)
        tril = jnp.tril(jnp.ones((c, c), F32))
        wm_c = wm_full[:, :c, :c] * tril
        eye = jnp.eye(seqs, dtype=F32)
        wbig_bf = jnp.einsum("ab,hts->hatbs", eye, wm_c).reshape(sgu_heads, seqs * c, seqs * c).astype(BF16)
        bt = jnp.tile(sgu_b[:, :c].T, (seqs, 1))
        ck, cv = stream["cache"]
        cache_len = ck.shape[0] // (n // t_new)
        h, hn, wk, wv = _mix_out_sample(sinks, q, k, v, ck, cv, su, sv, x2, wbig_bf, bt, aon, son, w_out_bf, nffn,
                                        n_heads=n_heads, group=group, sgu_heads=sgu_heads, seqs=seqs, t_new=t_new,
                                        cache_len=cache_len)
        k_state = wk.reshape(n // t_new, cache_len, kv_w)[:, -WINDOW:]
        v_state = wv.reshape(n // t_new, cache_len, kv_w)[:, -WINDOW:]
    ids, g = _peer_route(hn, wq_bf, k1_bf, k2_bf, tm=min(TM_IN, n))
    h2 = _peer_apply(ids.reshape(-1), h, hn, g, table, tb=PEER_TB)
    out = _ple(h2, p2, gn, gw_bf, pw_bf, tm=min(TM_IN, n))
    return out, k_state, v_state, sv


def kernel(x_prompt, x_sample, cache_k, cache_v, p_prompt, p_sample, norm_mix, w_in, q_norm, k_norm, sinks, sgu_norm,
           sgu_w, sgu_b, attn_out_norm, sgu_out_norm, w_out, norm_ffn, peer_wq, peer_k1, peer_k2, peer_u, peer_v,
           ple_w, ple_gate_norm, ple_gate_w):
    batch, seq, d = x_prompt.shape
    dec_batch, dec_seq, _ = x_sample.shape
    depth = w_in.shape[0]
    n_heads = sinks.shape[1]
    n_kv = cache_k.shape[3]
    group = n_heads // n_kv
    att_w = n_heads * HEAD_DIM
    kv_w = n_kv * HEAD_DIM
    sgu_heads = sgu_w.shape[1]
    sgu_width = sgu_out_norm.shape[1]
    cache_len = cache_k.shape[2]
    assert kv_w == LANES and seq % TM_IN == 0 and TM_IN % ATT_BLOCK == 0
    assert cache_len == WINDOW, "sample attention assumes a full window buffer"

    pos_p = jnp.arange(seq, dtype=F32)
    pos_s = PAST_LEN + jnp.arange(dec_seq, dtype=F32)
    rows_s = 16 * dec_seq
    rope_p = _rope_tables(pos_p)
    rope_s = tuple(jnp.tile(a, (rows_s // dec_seq, 1)) for a in _rope_tables(pos_s))
    ind, indt = _head_indicator(att_w + kv_w)

    hp = x_prompt.reshape(batch * seq, d)
    hs = x_sample.reshape(dec_batch * dec_seq, d)
    kp_l, vp_l, ks_l, vs_l, sv_l = [], [], [], [], []
    for i in range(depth):
        gain_qk = jnp.concatenate([jnp.tile(q_norm[i], n_heads), jnp.tile(k_norm[i], n_kv)])[None, :]
        lw = (norm_mix[i][None, :], w_in[i].astype(BF16), gain_qk, ind, indt, sinks[i], sgu_norm[i][None, :],
              sgu_w[i], sgu_b[i], attn_out_norm[i][None, :], sgu_out_norm[i][None, :], w_out[i].astype(BF16),
              norm_ffn[i][None, :], peer_wq[i].astype(BF16), peer_k1[i].astype(BF16), peer_k2[i].astype(BF16),
              _pack_tables(peer_u[i], peer_v[i]), ple_w[i].astype(BF16), ple_gate_norm[i][None, :],
              ple_gate_w[i].astype(BF16), (att_w, kv_w, sgu_width, n_heads, group, sgu_heads))
        prompt = dict(kind="prompt", batch=batch, seq=seq, rope=rope_p)
        sample = dict(kind="sample", rows=rows_s, t_new=dec_seq, rope=rope_s,
                      cache=(cache_k[i].reshape(dec_batch * cache_len, kv_w),
                             cache_v[i].reshape(dec_batch * cache_len, kv_w)))
        hp, kp, vp, _ = _layer(hp, p_prompt[i].reshape(batch * seq, -1), lw, prompt)
        hs, kss, vss, svs = _layer(hs, p_sample[i].reshape(dec_batch * dec_seq, -1), lw, sample)
        kp_l.append(kp.reshape(batch, WINDOW, n_kv, HEAD_DIM))
        vp_l.append(vp.reshape(batch, WINDOW, n_kv, HEAD_DIM))
        ks_l.append(kss.reshape(dec_batch, WINDOW, n_kv, HEAD_DIM))
        vs_l.append(vss.reshape(dec_batch, WINDOW, n_kv, HEAD_DIM))
        sv_l.append(svs.reshape(dec_batch, dec_seq, sgu_heads, -1))
    return (hp.reshape(batch, seq, d), hs.reshape(dec_batch, dec_seq, d), jnp.stack(kp_l), jnp.stack(vp_l),
            jnp.stack(ks_l), jnp.stack(vs_l), jnp.stack(sv_l))
```
